```python
import math
import jax, jax.numpy as jnp
from jax import lax
import numpy as np

D_MODEL = 2048
BATCH = 2
SEQ = 16384
DEPTH = 1

N_HEADS = 8
HEAD_DIM = 64
V_DIM = 2 * HEAD_DIM
ATTN_WIDTH = N_HEADS * V_DIM
Q_BLOCK = 128
CONV_CH = 1024
CONV_K = 31
D_FF = 5632
FFN_CONV_K = 3
PLE_DIM = 256
EPS = 1e-6
LN_EPS = 1e-5

Q_COLS = N_HEADS * 2 * HEAD_DIM
K_COLS = N_HEADS * 2 * HEAD_DIM
V_COLS = N_HEADS * V_DIM
GLU_COLS = 2 * CONV_CH
GATE_COLS = 2 * D_MODEL
IN_COLS = Q_COLS + K_COLS + V_COLS + GLU_COLS + GATE_COLS

kernel_name = "hybrid_diffattn_conformer_convffn_block"


def _rmsnorm(x, g):
    xf = x.astype(jnp.float32)
    y = xf * lax.rsqrt(jnp.mean(xf * xf, axis=-1, keepdims=True) + EPS)
    return (y * g.astype(jnp.float32)).astype(x.dtype)


def _layernorm(x, g, b):
    xf = x.astype(jnp.float32)
    mu = jnp.mean(xf, axis=-1, keepdims=True)
    var = jnp.mean(jnp.square(xf - mu), axis=-1, keepdims=True)
    y = (xf - mu) * lax.rsqrt(var + LN_EPS)
    return (y * g.astype(jnp.float32) + b.astype(jnp.float32)).astype(x.dtype)


def _causal_dwconv(x, w, b):
    k = w.shape[0]
    c = x.shape[-1]
    y = lax.conv_general_dilated(
        x, w[:, None, :].astype(x.dtype), window_strides=(1,), padding=[(k - 1, 0)],
        dimension_numbers=("NWC", "WIO", "NWC"), feature_group_count=c)
    return y + b.astype(x.dtype)


def _alibi_slopes():
    return jnp.exp2(-8.0 * jnp.arange(1, N_HEADS + 1, dtype=jnp.float32) / N_HEADS)


def _diff_attention(hq, hk, hv, lam, g_subln, lam_init):
    b, s = hq.shape[0], hq.shape[1]
    q = hq.reshape(b, s, N_HEADS, 2, HEAD_DIM).transpose(3, 0, 2, 1, 4)
    k = hk.reshape(b, s, N_HEADS, 2, HEAD_DIM).transpose(3, 0, 2, 1, 4)
    v = hv.reshape(b, s, N_HEADS, V_DIM).transpose(0, 2, 1, 3)
    n_blk = s // Q_BLOCK
    qb = q.reshape(2, b, N_HEADS, n_blk, Q_BLOCK, HEAD_DIM).transpose(3, 0, 1, 2, 4, 5)
    slopes = _alibi_slopes()
    k_pos = jnp.arange(s, dtype=jnp.int32)
    scale = HEAD_DIM ** -0.5

    def block(args):
        qi, i = args
        q_pos = i * Q_BLOCK + jnp.arange(Q_BLOCK, dtype=jnp.int32)
        dist = q_pos[:, None] - k_pos[None, :]
        bias = -slopes[:, None, None] * dist.astype(jnp.float32)[None]
        sc = jnp.einsum("nbhqd,nbhkd->nbhqk", qi, k).astype(jnp.float32) * scale + bias
        sc = jnp.where(dist >= 0, sc, -jnp.inf)
        pr = jax.nn.softmax(sc, axis=-1)
        a = pr[0] - lam * pr[1]
        return jnp.einsum("bhqk,bhke->bhqe", a.astype(v.dtype), v)

    out = lax.map(block, (qb, jnp.arange(n_blk, dtype=jnp.int32)))
    out = out.transpose(1, 0, 3, 2, 4).reshape(b, s, N_HEADS, V_DIM)
    out = _rmsnorm(out, g_subln) * (1.0 - lam_init)
    return out.reshape(b, s, ATTN_WIDTH)


def _conformer_conv(glu_in, conv_w, conv_b, ln_g, ln_b, w_br):
    a, g = jnp.split(glu_in, 2, axis=-1)
    u = a * jax.nn.sigmoid(g)
    u = _causal_dwconv(u, conv_w, conv_b)
    u = _layernorm(u, ln_g, ln_b)
    u = jax.nn.silu(u)
    return u @ w_br


def setup_inputs(seed: int = 0) -> dict:
    key = jax.random.key(seed)
    ks = jax.random.split(key, 26)
    f32 = jnp.float32
    L, D = DEPTH, D_MODEL

    def nrm(k, shape, fan_in):
        return jax.random.normal(k, shape, f32) * (fan_in ** -0.5)

    def gain(k, shape):
        return 1.0 + 0.02 * jax.random.normal(k, shape, f32)

    return {
        "x": jax.random.normal(ks[0], (BATCH, SEQ, D), f32),
        "p": jax.random.normal(ks[1], (DEPTH, BATCH, SEQ, PLE_DIM), f32),
        "g_mix": gain(ks[2], (L, D)),
        "w_in": nrm(ks[3], (L, D, IN_COLS), D),
        "lam_q1": 0.1 * jax.random.normal(ks[4], (L, HEAD_DIM), f32),
        "lam_k1": 0.1 * jax.random.normal(ks[5], (L, HEAD_DIM), f32),
        "lam_q2": 0.1 * jax.random.normal(ks[6], (L, HEAD_DIM), f32),
        "lam_k2": 0.1 * jax.random.normal(ks[7], (L, HEAD_DIM), f32),
        "g_subln": gain(ks[8], (L, V_DIM)),
        "w_attn_br": nrm(ks[9], (L, ATTN_WIDTH, D), ATTN_WIDTH),
        "conv_w": nrm(ks[10], (L, CONV_K, CONV_CH), CONV_K),
        "conv_b": 0.02 * jax.random.normal(ks[11], (L, CONV_CH), f32),
        "ln_g": gain(ks[12], (L, CONV_CH)),
        "ln_b": 0.02 * jax.random.normal(ks[13], (L, CONV_CH), f32),
        "w_conv_br": nrm(ks[14], (L, CONV_CH, D), CONV_CH),
        "w_o": nrm(ks[15], (L, D, D), D),
        "g_ffn": gain(ks[16], (L, D)),
        "w_up": nrm(ks[17], (L, D, 2 * D_FF), D),
        "ffn_conv_w": nrm(ks[18], (L, FFN_CONV_K, 2 * D_FF), FFN_CONV_K),
        "ffn_conv_b": 0.02 * jax.random.normal(ks[19], (L, 2 * D_FF), f32),
        "w_down": nrm(ks[20], (L, D_FF, D), D_FF),
        "g_ple": gain(ks[21], (L, D)),
        "w_ple_gate": nrm(ks[22], (L, D, D), D),
        "w_ple_proj": nrm(ks[23], (L, PLE_DIM, D), PLE_DIM),
        "g_final": gain(ks[24], (D,)),
    }


def reference(x, p, g_mix, w_in, lam_q1, lam_k1, lam_q2, lam_k2, g_subln, w_attn_br,
              conv_w, conv_b, ln_g, ln_b, w_conv_br, w_o, g_ffn, w_up, ffn_conv_w,
              ffn_conv_b, w_down, g_ple, w_ple_gate, w_ple_proj, g_final):
    splits = np.cumsum([Q_COLS, K_COLS, V_COLS, GLU_COLS, D_MODEL]).tolist()
    for l in range(DEPTH):
        lam_init = 0.8 - 0.6 * math.exp(-0.3 * l)
        lam = (jnp.exp(jnp.sum(lam_q1[l].astype(jnp.float32) * lam_k1[l].astype(jnp.float32)))
               - jnp.exp(jnp.sum(lam_q2[l].astype(jnp.float32) * lam_k2[l].astype(jnp.float32)))
               + lam_init)
        h = _rmsnorm(x, g_mix[l])
        z = h @ w_in[l]
        hq, hk, hv, glu_in, gate_a, gate_c = jnp.split(z, splits, axis=-1)
        attn = _diff_attention(hq, hk, hv, lam, g_subln[l], lam_init) @ w_attn_br[l]
        conv = _conformer_conv(glu_in, conv_w[l], conv_b[l], ln_g[l], ln_b[l], w_conv_br[l])
        merged = jax.nn.sigmoid(gate_a) * attn + jax.nn.sigmoid(gate_c) * conv
        x = x + merged @ w_o[l]
        h = _rmsnorm(x, g_ffn[l])
        u = _causal_dwconv(h @ w_up[l], ffn_conv_w[l], ffn_conv_b[l])
        ug, uv = jnp.split(u, 2, axis=-1)
        x = x + (jax.nn.silu(ug) * uv) @ w_down[l]
        h = _rmsnorm(x, g_ple[l])
        x = x + jax.nn.sigmoid(h @ w_ple_gate[l]) * (p[l].astype(x.dtype) @ w_ple_proj[l])
    return _rmsnorm(x, g_final)
```

```python
import functools
import math

import jax
import jax.numpy as jnp
from jax import lax
from jax.experimental import pallas as pl
from jax.experimental.pallas import tpu as pltpu

D_MODEL = 2048
N_HEADS = 8
HEAD_DIM = 64
V_DIM = 2 * HEAD_DIM
ATTN_WIDTH = N_HEADS * V_DIM
CONV_CH = 1024
CONV_K = 31
D_FF = 5632
FFN_CONV_K = 3
PLE_DIM = 256
EPS = 1e-6
LN_EPS = 1e-5
IN_COLS = 3 * ATTN_WIDTH + 2 * CONV_CH + 2 * D_MODEL

LANES = 128
BF16_ROWS = 16
CONV_HALO = 32
FFN_HALO = 16

F32 = jnp.float32
BF16 = jnp.bfloat16
MIB = 1024 * 1024


def _rms(x, g):
    ms = jnp.mean(x * x, axis=-1, keepdims=True)
    return x * lax.rsqrt(ms + EPS) * g


def _sigmoid(x):
    return 1.0 / (1.0 + jnp.exp(-x))


def _params(sem, vmem_mib):
    return pltpu.CompilerParams(dimension_semantics=sem, vmem_limit_bytes=vmem_mib * MIB)


def _inproj_body(x_ref, g_ref, w_ref, o_ref, h_ref):
    @pl.when(pl.program_id(1) == 0)
    def _():
        h_ref[...] = _rms(x_ref[...], g_ref[...]).astype(h_ref.dtype)

    o_ref[...] = jnp.dot(h_ref[...], w_ref[...],
                         preferred_element_type=F32).astype(o_ref.dtype)


def _inproj(x2d, g, w):
    m, d = x2d.shape
    n = w.shape[1]
    tm = min(1024, m)
    tn = 1024
    return pl.pallas_call(
        _inproj_body,
        grid=(m // tm, n // tn),
        in_specs=[pl.BlockSpec((tm, d), lambda i, j: (i, 0)),
                  pl.BlockSpec((1, d), lambda i, j: (0, 0)),
                  pl.BlockSpec((d, tn), lambda i, j: (0, j))],
        out_specs=pl.BlockSpec((tm, tn), lambda i, j: (i, j)),
        out_shape=jax.ShapeDtypeStruct((m, n), BF16),
        scratch_shapes=[pltpu.VMEM((tm, d), BF16)],
        compiler_params=_params(("parallel", "arbitrary"), 48),
        name="inproj",
    )(x2d, g, w)


def _attn_body(slopes_ref, q_ref, k_ref, v_ref, lam_ref, g_ref, o_ref,
               m_ref, l_ref, acc_ref, bias_ref, *, tq, lam_init):
    h = pl.program_id(1)
    qi = pl.program_id(2)
    slope = slopes_ref[h]

    row = lax.broadcasted_iota(jnp.int32, (tq, tq), 0)
    col = lax.broadcasted_iota(jnp.int32, (tq, tq), 1)
    bias_ref[...] = slope * (col - row).astype(F32)

    q = q_ref[...]
    lane = lax.broadcasted_iota(jnp.int32, q.shape, 1)
    qs = q * (HEAD_DIM ** -0.5)
    zero = jnp.zeros_like(qs)
    qn = (jnp.where(lane < HEAD_DIM, qs, zero), jnp.where(lane >= HEAD_DIM, qs, zero))

    m_ref[...] = jnp.full(m_ref.shape, -jnp.inf, F32)
    l_ref[...] = jnp.zeros(l_ref.shape, F32)
    acc_ref[...] = jnp.zeros(acc_ref.shape, F32)

    def tile(kb, masked):
        start = pl.multiple_of(kb * tq, tq)
        k = k_ref[pl.ds(start, tq), :]
        v = v_ref[pl.ds(start, tq), :]
        c = -slope * ((qi - kb) * tq).astype(F32)
        bias = bias_ref[...]
        for n in range(2):
            s = lax.dot_general(qn[n], k, (((1,), (1,)), ((), ())),
                                preferred_element_type=F32) + bias
            if masked:
                s = jnp.where(col <= row, s, -jnp.inf)
            m_prev = m_ref[n]
            m_cur = jnp.max(s, axis=1, keepdims=True) + c
            m_next = jnp.maximum(m_prev, m_cur)
            p = jnp.exp(s - jnp.tile(m_next - c, (1, tq // LANES)))
            alpha = jnp.exp(m_prev - m_next)
            l_ref[n] = alpha * l_ref[n] + jnp.sum(p, axis=1, keepdims=True)
            m_ref[n] = m_next
            acc_ref[n] = alpha * acc_ref[n] + jnp.dot(
                p.astype(v.dtype), v, preferred_element_type=F32)

    def body(kb, carry):
        tile(kb, False)
        return carry

    lax.fori_loop(0, qi, body, 0)
    tile(qi, True)

    lv = lam_ref[...]
    e1 = jnp.exp(jnp.sum(lv[0:1] * lv[1:2], axis=1, keepdims=True))
    e2 = jnp.exp(jnp.sum(lv[2:3] * lv[3:4], axis=1, keepdims=True))
    lam = e1 - e2 + lam_init
    o = acc_ref[0] / l_ref[0] - lam * (acc_ref[1] / l_ref[1])
    o_ref[...] = (_rms(o, g_ref[...]) * (1.0 - lam_init)).astype(o_ref.dtype)


def _attention(z3, slopes, lam_vecs, g_subln, lam_init):
    b, s, _ = z3.shape
    tq = min(512, s)
    kcol = ATTN_WIDTH // V_DIM
    vcol = 2 * ATTN_WIDTH // V_DIM
    return pl.pallas_call(
        functools.partial(_attn_body, tq=tq, lam_init=lam_init),
        grid=(b, N_HEADS, s // tq),
        in_specs=[pl.BlockSpec(memory_space=pltpu.SMEM),
                  pl.BlockSpec((None, tq, V_DIM), lambda bi, h, i: (bi, i, h)),
                  pl.BlockSpec((None, s, V_DIM), lambda bi, h, i: (bi, 0, kcol + h)),
                  pl.BlockSpec((None, s, V_DIM), lambda bi, h, i: (bi, 0, vcol + h)),
                  pl.BlockSpec((4, HEAD_DIM), lambda bi, h, i: (0, 0)),
                  pl.BlockSpec((1, V_DIM), lambda bi, h, i: (0, 0))],
        out_specs=pl.BlockSpec((None, tq, V_DIM), lambda bi, h, i: (bi, i, h)),
        out_shape=jax.ShapeDtypeStruct((b, s, ATTN_WIDTH), BF16),
        scratch_shapes=[pltpu.VMEM((2, tq, LANES), F32),
                        pltpu.VMEM((2, tq, LANES), F32),
                        pltpu.VMEM((2, tq, V_DIM), F32),
                        pltpu.VMEM((tq, tq), F32)],
        compiler_params=_params(("parallel", "parallel", "arbitrary"), 48),
        name="attn",
    )(slopes, z3, z3, z3, lam_vecs, g_subln)


def _convbr_body(a_ref, g_ref, ah_ref, gh_ref, w_ref, b_ref, lng_ref, lnb_ref, o_ref,
                 u_ref, y_ref, *, ts, rc):
    i = pl.program_id(1)
    u_ref[pl.ds(CONV_HALO, ts), :] = a_ref[...].astype(F32) * _sigmoid(g_ref[...].astype(F32))
    uh = ah_ref[...].astype(F32) * _sigmoid(gh_ref[...].astype(F32))
    u_ref[pl.ds(0, CONV_HALO), :] = jnp.where(i > 0, uh, 0.0)

    first = CONV_HALO - (CONV_K - 1)
    for c in range(CONV_CH // LANES):
        cs = pl.ds(c * LANES, LANES)
        for r in range(ts // rc):
            acc = jnp.broadcast_to(b_ref[:, cs], (rc, LANES))
            for k in range(CONV_K):
                acc = acc + w_ref[pl.ds(k, 1), cs] * u_ref[pl.ds(r * rc + first + k, rc), cs]
            y_ref[pl.ds(r * rc, rc), cs] = acc

    y = y_ref[...]
    mu = jnp.mean(y, axis=-1, keepdims=True)
    yc = y - mu
    var = jnp.mean(yc * yc, axis=-1, keepdims=True)
    yn = yc * lax.rsqrt(var + LN_EPS) * lng_ref[...] + lnb_ref[...]
    o_ref[...] = (yn * _sigmoid(yn)).astype(o_ref.dtype)


def _convbr(z3, conv_w, conv_b, ln_g, ln_b):
    b, s, _ = z3.shape
    ts = min(512, s)
    rc = 128
    acol = 3 * ATTN_WIDTH // CONV_CH
    gcol = acol + 1
    hb = ts // CONV_HALO
    halo = lambda col: (lambda bi, i: (bi, jnp.maximum(i * hb - 1, 0), col))
    return pl.pallas_call(
        functools.partial(_convbr_body, ts=ts, rc=rc),
        grid=(b, s // ts),
        in_specs=[pl.BlockSpec((None, ts, CONV_CH), lambda bi, i: (bi, i, acol)),
                  pl.BlockSpec((None, ts, CONV_CH), lambda bi, i: (bi, i, gcol)),
                  pl.BlockSpec((None, CONV_HALO, CONV_CH), halo(acol)),
                  pl.BlockSpec((None, CONV_HALO, CONV_CH), halo(gcol)),
                  pl.BlockSpec((CONV_K, CONV_CH), lambda bi, i: (0, 0)),
                  pl.BlockSpec((1, CONV_CH), lambda bi, i: (0, 0)),
                  pl.BlockSpec((1, CONV_CH), lambda bi, i: (0, 0)),
                  pl.BlockSpec((1, CONV_CH), lambda bi, i: (0, 0))],
        out_specs=pl.BlockSpec((None, ts, CONV_CH), lambda bi, i: (bi, i, 0)),
        out_shape=jax.ShapeDtypeStruct((b, s, CONV_CH), BF16),
        scratch_shapes=[pltpu.VMEM((ts + CONV_HALO, CONV_CH), F32),
                        pltpu.VMEM((ts, CONV_CH), F32)],
        compiler_params=_params(("parallel", "arbitrary"), 32),
        name="convbr",
    )(z3, z3, z3, z3, conv_w, conv_b, ln_g, ln_b)


def _merge_body(x_ref, at_ref, cb_ref, ga0_ref, ga1_ref, gc0_ref, gc1_ref,
                wa_ref, wc_ref, wo_ref, o_ref):
    att = jnp.dot(at_ref[...], wa_ref[...], preferred_element_type=F32)
    cv = jnp.dot(cb_ref[...], wc_ref[...], preferred_element_type=F32)
    ga = jnp.concatenate([ga0_ref[...], ga1_ref[...]], axis=1).astype(F32)
    gc = jnp.concatenate([gc0_ref[...], gc1_ref[...]], axis=1).astype(F32)
    merged = (_sigmoid(ga) * att + _sigmoid(gc) * cv).astype(BF16)
    o_ref[...] = x_ref[...] + jnp.dot(merged, wo_ref[...], preferred_element_type=F32)


def _merge(x2d, z, attn2d, conv2d, wa, wc, wo):
    m, d = x2d.shape
    tm = min(512, m)
    gw = ATTN_WIDTH
    g0 = (3 * ATTN_WIDTH + 2 * CONV_CH) // gw
    gate = lambda col: pl.BlockSpec((tm, gw), lambda i: (i, col))
    const = lambda shape: pl.BlockSpec(shape, lambda i: (0, 0), pipeline_mode=pl.Buffered(1))
    return pl.pallas_call(
        _merge_body,
        grid=(m // tm,),
        in_specs=[pl.BlockSpec((tm, d), lambda i: (i, 0)),
                  pl.BlockSpec((tm, ATTN_WIDTH), lambda i: (i, 0)),
                  pl.BlockSpec((tm, CONV_CH), lambda i: (i, 0)),
                  gate(g0), gate(g0 + 1), gate(g0 + 2), gate(g0 + 3),
                  const((ATTN_WIDTH, d)), const((CONV_CH, d)), const((d, d))],
        out_specs=pl.BlockSpec((tm, d), lambda i: (i, 0)),
        out_shape=jax.ShapeDtypeStruct((m, d), F32),
        compiler_params=_params(("parallel",), 56),
        name="merge",
    )(x2d, attn2d, conv2d, z, z, z, z, wa, wc, wo)


def _ffn_body(x_ref, xh_ref, g_ref, wg_ref, wv_ref, cwg_ref, cwv_ref, cbg_ref, cbv_ref,
              wd_ref, o_ref, h_ref, acc_ref, *, tm, tiles_per_seq):
    i = pl.program_id(0)
    j = pl.program_id(1)

    @pl.when(j == 0)
    def _():
        x = x_ref[...]
        h_ref[pl.ds(FFN_HALO, tm), :] = _rms(x, g_ref[...]).astype(h_ref.dtype)
        hh = _rms(xh_ref[...], g_ref[...])
        h_ref[pl.ds(0, FFN_HALO), :] = jnp.where(i % tiles_per_seq != 0, hh, 0.0).astype(h_ref.dtype)
        acc_ref[...] = x

    h = h_ref[...]

    def conv(w_ref, cw_ref, cb_ref):
        u = jnp.dot(h, w_ref[...], preferred_element_type=F32)
        y = cw_ref[pl.ds(2, 1), :] * u
        y = y + cw_ref[pl.ds(1, 1), :] * pltpu.roll(u, 1, axis=0)
        y = y + cw_ref[pl.ds(0, 1), :] * pltpu.roll(u, 2, axis=0)
        return y[FFN_HALO:, :] + cb_ref[...]

    yg = conv(wg_ref, cwg_ref, cbg_ref)
    yv = conv(wv_ref, cwv_ref, cbv_ref)
    act = (yg * _sigmoid(yg) * yv).astype(BF16)
    acc_ref[...] += jnp.dot(act, wd_ref[...], preferred_element_type=F32)

    @pl.when(j == pl.num_programs(1) - 1)
    def _():
        o_ref[...] = acc_ref[...]


def _ffn(x2d, seq, g, w_up, cw, cb, w_down):
    m, d = x2d.shape
    tm = min(512, seq)
    tf = 512
    nf = D_FF // tf
    hb = tm // FFN_HALO
    return pl.pallas_call(
        functools.partial(_ffn_body, tm=tm, tiles_per_seq=seq // tm),
        grid=(m // tm, nf),
        in_specs=[pl.BlockSpec((tm, d), lambda i, j: (i, 0)),
                  pl.BlockSpec((FFN_HALO, d), lambda i, j: (jnp.maximum(i * hb - 1, 0), 0)),
                  pl.BlockSpec((1, d), lambda i, j: (0, 0)),
                  pl.BlockSpec((d, tf), lambda i, j: (0, j)),
                  pl.BlockSpec((d, tf), lambda i, j: (0, nf + j)),
                  pl.BlockSpec((FFN_CONV_K, tf), lambda i, j: (0, j)),
                  pl.BlockSpec((FFN_CONV_K, tf), lambda i, j: (0, nf + j)),
                  pl.BlockSpec((1, tf), lambda i, j: (0, j)),
                  pl.BlockSpec((1, tf), lambda i, j: (0, nf + j)),
                  pl.BlockSpec((tf, d), lambda i, j: (j, 0))],
        out_specs=pl.BlockSpec((tm, d), lambda i, j: (i, 0)),
        out_shape=jax.ShapeDtypeStruct((m, d), F32),
        scratch_shapes=[pltpu.VMEM((tm + FFN_HALO, d), BF16),
                        pltpu.VMEM((tm, d), F32)],
        compiler_params=_params(("parallel", "arbitrary"), 48),
        name="ffn",
    )(x2d, x2d, g, w_up, w_up, cw, cw, cb, cb, w_down)


def _ple_body(x_ref, p_ref, g_ref, wg_ref, wp_ref, gf_ref, o_ref, *, final):
    x = x_ref[...]
    h = _rms(x, g_ref[...]).astype(BF16)
    gate = _sigmoid(jnp.dot(h, wg_ref[...], preferred_element_type=F32))
    proj = jnp.dot(p_ref[...].astype(BF16), wp_ref[...], preferred_element_type=F32)
    y = x + gate * proj
    if final:
        y = _rms(y, gf_ref[...])
    o_ref[...] = y


def _ple(x2d, p2d, g, wg, wp, g_final, final):
    m, d = x2d.shape
    tm = min(512, m)
    const = lambda shape: pl.BlockSpec(shape, lambda i: (0, 0), pipeline_mode=pl.Buffered(1))
    return pl.pallas_call(
        functools.partial(_ple_body, final=final),
        grid=(m // tm,),
        in_specs=[pl.BlockSpec((tm, d), lambda i: (i, 0)),
                  pl.BlockSpec((tm, PLE_DIM), lambda i: (i, 0)),
                  pl.BlockSpec((1, d), lambda i: (0, 0)),
                  const((d, d)), const((PLE_DIM, d)),
                  pl.BlockSpec((1, d), lambda i: (0, 0))],
        out_specs=pl.BlockSpec((tm, d), lambda i: (i, 0)),
        out_shape=jax.ShapeDtypeStruct((m, d), F32),
        compiler_params=_params(("parallel",), 40),
        name="ple",
    )(x2d, p2d, g, wg, wp, g_final)


def kernel(x, p, g_mix, w_in, lam_q1, lam_k1, lam_q2, lam_k2, g_subln, w_attn_br,
           conv_w, conv_b, ln_g, ln_b, w_conv_br, w_o, g_ffn, w_up, ffn_conv_w,
           ffn_conv_b, w_down, g_ple, w_ple_gate, w_ple_proj, g_final):
    b, s, d = x.shape
    depth = w_in.shape[0]
    m = b * s
    row = lambda a: a.reshape(1, -1)
    slopes = jnp.exp2(-8.0 * jnp.arange(1, N_HEADS + 1, dtype=F32) / N_HEADS)
    x2d = x.reshape(m, d)
    for l in range(depth):
        lam_init = 0.8 - 0.6 * math.exp(-0.3 * l)
        lam_vecs = jnp.stack([lam_q1[l], lam_k1[l], lam_q2[l], lam_k2[l]]).astype(F32)
        z = _inproj(x2d, row(g_mix[l]), w_in[l].astype(BF16))
        z3 = z.reshape(b, s, IN_COLS)
        attn = _attention(z3, slopes, lam_vecs, row(g_subln[l]), lam_init)
        conv = _convbr(z3, conv_w[l], row(conv_b[l]), row(ln_g[l]), row(ln_b[l]))
        x2d = _merge(x2d, z, attn.reshape(m, ATTN_WIDTH), conv.reshape(m, CONV_CH),
                     w_attn_br[l].astype(BF16), w_conv_br[l].astype(BF16), w_o[l].astype(BF16))
        x2d = _ffn(x2d, s, row(g_ffn[l]), w_up[l].astype(BF16), ffn_conv_w[l],
                   row(ffn_conv_b[l]), w_down[l].astype(BF16))
        x2d = _ple(x2d, p[l].reshape(m, PLE_DIM), row(g_ple[l]), w_ple_gate[l].astype(BF16),
                   w_ple_proj[l].astype(BF16), row(g_final), l == depth - 1)
    return x2d.reshape(b, s, d)
```

```python
import functools
import math

import jax
import jax.numpy as jnp
from jax import lax
from jax.experimental import pallas as pl
from jax.experimental.pallas import tpu as pltpu

D_MODEL = 2048
N_HEADS = 8
HEAD_DIM = 64
V_DIM = 2 * HEAD_DIM
ATTN_WIDTH = N_HEADS * V_DIM
CONV_CH = 1024
CONV_K = 31
D_FF = 5632
FFN_CONV_K = 3
PLE_DIM = 256
EPS = 1e-6
LN_EPS = 1e-5
IN_COLS = 3 * ATTN_WIDTH + 2 * CONV_CH + 2 * D_MODEL

LANES = 128
BF16_ROWS = 16
KEY_CHUNK = 256
CONV_HALO = 32
FFN_HALO = 16

F32 = jnp.float32
BF16 = jnp.bfloat16
MIB = 1024 * 1024


def _rms(x, g):
    ms = jnp.mean(x * x, axis=-1, keepdims=True)
    return x * lax.rsqrt(ms + EPS) * g


def _sigmoid(x):
    return 1.0 / (1.0 + jnp.exp(-x))


def _params(sem, vmem_mib, flags=None):
    return pltpu.CompilerParams(dimension_semantics=sem, vmem_limit_bytes=vmem_mib * MIB,
                                flags=flags)


def _inproj_body(x_ref, g_ref, w_ref, o_ref, h_ref):
    @pl.when(pl.program_id(1) == 0)
    def _():
        h_ref[...] = _rms(x_ref[...], g_ref[...]).astype(h_ref.dtype)

    o_ref[...] = jnp.dot(h_ref[...], w_ref[...],
                         preferred_element_type=F32).astype(o_ref.dtype)


def _inproj(x2d, g, w):
    m, d = x2d.shape
    n = w.shape[1]
    tm = min(1024, m)
    tn = 1024
    return pl.pallas_call(
        _inproj_body,
        grid=(m // tm, n // tn),
        in_specs=[pl.BlockSpec((tm, d), lambda i, j: (i, 0)),
                  pl.BlockSpec((1, d), lambda i, j: (0, 0)),
                  pl.BlockSpec((d, tn), lambda i, j: (0, j))],
        out_specs=pl.BlockSpec((tm, tn), lambda i, j: (i, j)),
        out_shape=jax.ShapeDtypeStruct((m, n), BF16),
        scratch_shapes=[pltpu.VMEM((tm, d), BF16)],
        compiler_params=_params(("parallel", "arbitrary"), 48),
        name="inproj",
    )(x2d, g, w)


def _pos_columns(sp_ref, h, idx, lane, key_side):
    hi = (idx >> 7).astype(F32)
    lo = (idx & (LANES - 1)).astype(F32)
    out = jnp.zeros(lane.shape, F32)
    for p in range(3):
        sp = sp_ref[h, p]
        if key_side:
            cols = (hi, lo, -float(LANES) * sp, -sp)
        else:
            cols = (float(LANES) * sp, sp, hi, lo)
        for t, val in enumerate(cols):
            out = jnp.where(lane == 4 * p + t, val, out)
    return out.astype(BF16)


def _attn_body(sp_ref, q_ref, k_ref, v_ref, lam_ref, g_ref, o_ref,
               qa_ref, ka_ref, va_ref, s_ref, p_ref, mx_ref, m_ref, alpha_ref, acc_ref,
               *, tq, seq, lam_init):
    h = pl.program_id(1)
    qi = pl.program_id(2)
    nt = (((1,), (1,)), ((), ()))

    @pl.when(qi == 0)
    def _():
        def fill(c, carry):
            rows = pl.ds(pl.multiple_of(c * tq, tq), tq)
            lane = lax.broadcasted_iota(jnp.int32, (tq, LANES), 1)
            idx = lax.broadcasted_iota(jnp.int32, (tq, LANES), 0) + c * tq
            ka_ref[rows, pl.ds(0, V_DIM)] = k_ref[rows, :]
            ka_ref[rows, pl.ds(V_DIM, LANES)] = _pos_columns(sp_ref, h, idx, lane, True)
            va_ref[rows, pl.ds(0, V_DIM)] = v_ref[rows, :]
            va_ref[rows, pl.ds(V_DIM, LANES)] = jnp.ones((tq, LANES), BF16)
            return carry
        lax.fori_loop(0, seq // tq, fill, 0)

    q = q_ref[...]
    lane = lax.broadcasted_iota(jnp.int32, q.shape, 1)
    idx = lax.broadcasted_iota(jnp.int32, q.shape, 0) + qi * tq
    qs = q * (HEAD_DIM ** -0.5 * math.log2(math.e))
    zero = jnp.zeros_like(qs)
    qpos = _pos_columns(sp_ref, h, idx, lane, False)
    qa_ref[0, :, pl.ds(0, V_DIM)] = jnp.where(lane < HEAD_DIM, qs, zero)
    qa_ref[1, :, pl.ds(0, V_DIM)] = jnp.where(lane >= HEAD_DIM, qs, zero)
    qa_ref[0, :, pl.ds(V_DIM, LANES)] = qpos
    qa_ref[1, :, pl.ds(V_DIM, LANES)] = qpos

    m_ref[...] = jnp.full(m_ref.shape, -jnp.inf, F32)
    acc_ref[...] = jnp.zeros(acc_ref.shape, F32)

    def tile_rows(kb):
        return pl.ds(pl.multiple_of(kb * tq, tq), tq)

    def scores(n, kb, diagonal):
        s = lax.dot_general(qa_ref[n], ka_ref[tile_rows(kb), :], nt,
                            preferred_element_type=F32)
        if diagonal is not False:
            row = lax.broadcasted_iota(jnp.int32, s.shape, 0)
            col = lax.broadcasted_iota(jnp.int32, s.shape, 1)
            slack = 0 if diagonal is True else jnp.where(diagonal, 0, tq)
            s = jnp.where(col <= row + slack, s, -jnp.inf)
        s_ref[n] = s
        mx = s[:, :LANES]
        for c0 in range(LANES, tq, LANES):
            mx = jnp.maximum(mx, s[:, c0:c0 + LANES])
        mx_ref[n] = mx

    def softmax(n):
        m_prev = m_ref[n]
        m_next = jnp.maximum(m_prev, jnp.max(mx_ref[n], axis=1, keepdims=True))
        alpha_ref[n] = jnp.exp2(m_prev - m_next)
        m_ref[n] = m_next
        for c0 in range(0, tq, KEY_CHUNK):
            p_ref[n, :, pl.ds(c0, KEY_CHUNK)] = jnp.exp2(
                (s_ref[n, :, pl.ds(c0, KEY_CHUNK)]
                 - jnp.tile(m_next, (1, KEY_CHUNK // LANES))).astype(BF16))

    def accumulate(n, kb):
        acc_ref[n] = jnp.tile(alpha_ref[n], (1, 2)) * acc_ref[n] + jnp.dot(
            p_ref[n], va_ref[tile_rows(kb), :], preferred_element_type=F32)

    def step(kb, last):
        softmax(1)
        accumulate(0, kb - 1)
        scores(0, kb, last)
        softmax(0)
        accumulate(1, kb - 1)
        scores(1, kb, last)

    scores(0, 0, qi == 0)
    scores(1, 0, qi == 0)
    softmax(0)

    def body(kb, carry):
        step(kb, False)
        return carry

    lax.fori_loop(1, qi, body, 0)

    @pl.when(qi > 0)
    def _():
        step(qi, True)

    softmax(1)
    accumulate(0, qi)
    accumulate(1, qi)

    lv = lam_ref[...]
    e1 = jnp.exp(jnp.sum(lv[0:1] * lv[1:2], axis=1, keepdims=True))
    e2 = jnp.exp(jnp.sum(lv[2:3] * lv[3:4], axis=1, keepdims=True))
    lam = e1 - e2 + lam_init
    a0 = acc_ref[0]
    a1 = acc_ref[1]
    o = a0[:, :V_DIM] / a0[:, V_DIM:] - lam * (a1[:, :V_DIM] / a1[:, V_DIM:])
    o_ref[...] = (_rms(o, g_ref[...]) * (1.0 - lam_init)).astype(o_ref.dtype)


def _split_bf16(x):
    hi = x.astype(BF16).astype(F32)
    mid = (x - hi).astype(BF16).astype(F32)
    lo = (x - hi - mid).astype(BF16).astype(F32)
    return jnp.stack([hi, mid, lo], axis=-1)


def _attention(z3, slopes, lam_vecs, g_subln, lam_init):
    b, s, _ = z3.shape
    tq = min(1024, s)
    assert s % tq == 0 and s <= LANES * 256
    kcol = ATTN_WIDTH // V_DIM
    vcol = 2 * ATTN_WIDTH // V_DIM
    sp = _split_bf16(slopes * math.log2(math.e))
    once = lambda col: pl.BlockSpec((None, s, V_DIM), lambda bi, h, i: (bi, 0, col + h),
                                    pipeline_mode=pl.Buffered(1))
    return pl.pallas_call(
        functools.partial(_attn_body, tq=tq, seq=s, lam_init=lam_init),
        grid=(b, N_HEADS, s // tq),
        in_specs=[pl.BlockSpec(memory_space=pltpu.SMEM),
                  pl.BlockSpec((None, tq, V_DIM), lambda bi, h, i: (bi, i, h)),
                  once(kcol), once(vcol),
                  pl.BlockSpec((4, HEAD_DIM), lambda bi, h, i: (0, 0)),
                  pl.BlockSpec((1, V_DIM), lambda bi, h, i: (0, 0))],
        out_specs=pl.BlockSpec((None, tq, V_DIM), lambda bi, h, i: (bi, i, h)),
        out_shape=jax.ShapeDtypeStruct((b, s, ATTN_WIDTH), BF16),
        scratch_shapes=[pltpu.VMEM((2, tq, V_DIM + LANES), BF16),
                        pltpu.VMEM((s, V_DIM + LANES), BF16),
                        pltpu.VMEM((s, V_DIM + LANES), BF16),
                        pltpu.VMEM((2, tq, tq), F32),
                        pltpu.VMEM((2, tq, tq), BF16),
                        pltpu.VMEM((2, tq, LANES), F32),
                        pltpu.VMEM((2, tq, LANES), F32),
                        pltpu.VMEM((2, tq, LANES), F32),
                        pltpu.VMEM((2, tq, V_DIM + LANES), F32)],
        compiler_params=_params(("parallel", "parallel", "arbitrary"), 56),
        name="attn",
    )(sp, z3, z3, z3, lam_vecs, g_subln)


def _convbr_body(a_ref, g_ref, ah_ref, gh_ref, w_ref, b_ref, lng_ref, lnb_ref, o_ref,
                 u_ref, y_ref, *, ts, rc):
    i = pl.program_id(1)
    u_ref[pl.ds(CONV_HALO, ts), :] = a_ref[...].astype(F32) * _sigmoid(g_ref[...].astype(F32))
    uh = ah_ref[...].astype(F32) * _sigmoid(gh_ref[...].astype(F32))
    u_ref[pl.ds(0, CONV_HALO), :] = jnp.where(i > 0, uh, 0.0)

    first = CONV_HALO - (CONV_K - 1)
    for c in range(CONV_CH // LANES):
        cs = pl.ds(c * LANES, LANES)
        for r in range(ts // rc):
            acc = jnp.broadcast_to(b_ref[:, cs], (rc, LANES))
            for k in range(CONV_K):
                acc = acc + w_ref[pl.ds(k, 1), cs] * u_ref[pl.ds(r * rc + first + k, rc), cs]
            y_ref[pl.ds(r * rc, rc), cs] = acc

    y = y_ref[...]
    mu = jnp.mean(y, axis=-1, keepdims=True)
    yc = y - mu
    var = jnp.mean(yc * yc, axis=-1, keepdims=True)
    yn = yc * lax.rsqrt(var + LN_EPS) * lng_ref[...] + lnb_ref[...]
    o_ref[...] = (yn * _sigmoid(yn)).astype(o_ref.dtype)


def _convbr(z3, conv_w, conv_b, ln_g, ln_b):
    b, s, _ = z3.shape
    ts = min(512, s)
    rc = 128
    acol = 3 * ATTN_WIDTH // CONV_CH
    gcol = acol + 1
    hb = ts // CONV_HALO
    halo = lambda col: (lambda bi, i: (bi, jnp.maximum(i * hb - 1, 0), col))
    return pl.pallas_call(
        functools.partial(_convbr_body, ts=ts, rc=rc),
        grid=(b, s // ts),
        in_specs=[pl.BlockSpec((None, ts, CONV_CH), lambda bi, i: (bi, i, acol)),
                  pl.BlockSpec((None, ts, CONV_CH), lambda bi, i: (bi, i, gcol)),
                  pl.BlockSpec((None, CONV_HALO, CONV_CH), halo(acol)),
                  pl.BlockSpec((None, CONV_HALO, CONV_CH), halo(gcol)),
                  pl.BlockSpec((CONV_K, CONV_CH), lambda bi, i: (0, 0)),
                  pl.BlockSpec((1, CONV_CH), lambda bi, i: (0, 0)),
                  pl.BlockSpec((1, CONV_CH), lambda bi, i: (0, 0)),
                  pl.BlockSpec((1, CONV_CH), lambda bi, i: (0, 0))],
        out_specs=pl.BlockSpec((None, ts, CONV_CH), lambda bi, i: (bi, i, 0)),
        out_shape=jax.ShapeDtypeStruct((b, s, CONV_CH), BF16),
        scratch_shapes=[pltpu.VMEM((ts + CONV_HALO, CONV_CH), F32),
                        pltpu.VMEM((ts, CONV_CH), F32)],
        compiler_params=_params(("parallel", "arbitrary"), 32),
        name="convbr",
    )(z3, z3, z3, z3, conv_w, conv_b, ln_g, ln_b)


def _merge_body(x_ref, at_ref, cb_ref, ga0_ref, ga1_ref, gc0_ref, gc1_ref,
                wa_ref, wc_ref, wo_ref, o_ref):
    att = jnp.dot(at_ref[...], wa_ref[...], preferred_element_type=F32)
    cv = jnp.dot(cb_ref[...], wc_ref[...], preferred_element_type=F32)
    ga = jnp.concatenate([ga0_ref[...], ga1_ref[...]], axis=1).astype(F32)
    gc = jnp.concatenate([gc0_ref[...], gc1_ref[...]], axis=1).astype(F32)
    merged = (_sigmoid(ga) * att + _sigmoid(gc) * cv).astype(BF16)
    o_ref[...] = x_ref[...] + jnp.dot(merged, wo_ref[...], preferred_element_type=F32)


def _merge(x2d, z, attn2d, conv2d, wa, wc, wo):
    m, d = x2d.shape
    tm = min(512, m)
    gw = ATTN_WIDTH
    g0 = (3 * ATTN_WIDTH + 2 * CONV_CH) // gw
    gate = lambda col: pl.BlockSpec((tm, gw), lambda i: (i, col))
    const = lambda shape: pl.BlockSpec(shape, lambda i: (0, 0), pipeline_mode=pl.Buffered(1))
    return pl.pallas_call(
        _merge_body,
        grid=(m // tm,),
        in_specs=[pl.BlockSpec((tm, d), lambda i: (i, 0)),
                  pl.BlockSpec((tm, ATTN_WIDTH), lambda i: (i, 0)),
                  pl.BlockSpec((tm, CONV_CH), lambda i: (i, 0)),
                  gate(g0), gate(g0 + 1), gate(g0 + 2), gate(g0 + 3),
                  const((ATTN_WIDTH, d)), const((CONV_CH, d)), const((d, d))],
        out_specs=pl.BlockSpec((tm, d), lambda i: (i, 0)),
        out_shape=jax.ShapeDtypeStruct((m, d), F32),
        compiler_params=_params(("parallel",), 56),
        name="merge",
    )(x2d, attn2d, conv2d, z, z, z, z, wa, wc, wo)


def _ffn_body(x_ref, xh_ref, g_ref, wg_ref, wv_ref, cwg_ref, cwv_ref, cbg_ref, cbv_ref,
              wd_ref, o_ref, h_ref, acc_ref, *, tm, tiles_per_seq):
    i = pl.program_id(0)
    j = pl.program_id(1)

    @pl.when(j == 0)
    def _():
        x = x_ref[...]
        h_ref[pl.ds(FFN_HALO, tm), :] = _rms(x, g_ref[...]).astype(h_ref.dtype)
        hh = _rms(xh_ref[...], g_ref[...])
        h_ref[pl.ds(0, FFN_HALO), :] = jnp.where(i % tiles_per_seq != 0, hh, 0.0).astype(h_ref.dtype)
        acc_ref[...] = x

    h = h_ref[...]

    def conv(w_ref, cw_ref, cb_ref):
        u = jnp.dot(h, w_ref[...], preferred_element_type=F32)
        y = cw_ref[pl.ds(2, 1), :] * u
        y = y + cw_ref[pl.ds(1, 1), :] * pltpu.roll(u, 1, axis=0)
        y = y + cw_ref[pl.ds(0, 1), :] * pltpu.roll(u, 2, axis=0)
        return y[FFN_HALO:, :] + cb_ref[...]

    yg = conv(wg_ref, cwg_ref, cbg_ref)
    yv = conv(wv_ref, cwv_ref, cbv_ref)
    act = (yg * _sigmoid(yg) * yv).astype(BF16)
    acc_ref[...] += jnp.dot(act, wd_ref[...], preferred_element_type=F32)

    @pl.when(j == pl.num_programs(1) - 1)
    def _():
        o_ref[...] = acc_ref[...]


def _ffn(x2d, seq, g, w_up, cw, cb, w_down):
    m, d = x2d.shape
    tm = min(512, seq)
    tf = 512
    nf = D_FF // tf
    hb = tm // FFN_HALO
    return pl.pallas_call(
        functools.partial(_ffn_body, tm=tm, tiles_per_seq=seq // tm),
        grid=(m // tm, nf),
        in_specs=[pl.BlockSpec((tm, d), lambda i, j: (i, 0)),
                  pl.BlockSpec((FFN_HALO, d), lambda i, j: (jnp.maximum(i * hb - 1, 0), 0)),
                  pl.BlockSpec((1, d), lambda i, j: (0, 0)),
                  pl.BlockSpec((d, tf), lambda i, j: (0, j)),
                  pl.BlockSpec((d, tf), lambda i, j: (0, nf + j)),
                  pl.BlockSpec((FFN_CONV_K, tf), lambda i, j: (0, j)),
                  pl.BlockSpec((FFN_CONV_K, tf), lambda i, j: (0, nf + j)),
                  pl.BlockSpec((1, tf), lambda i, j: (0, j)),
                  pl.BlockSpec((1, tf), lambda i, j: (0, nf + j)),
                  pl.BlockSpec((tf, d), lambda i, j: (j, 0))],
        out_specs=pl.BlockSpec((tm, d), lambda i, j: (i, 0)),
        out_shape=jax.ShapeDtypeStruct((m, d), F32),
        scratch_shapes=[pltpu.VMEM((tm + FFN_HALO, d), BF16),
                        pltpu.VMEM((tm, d), F32)],
        compiler_params=_params(("parallel", "arbitrary"), 48),
        name="ffn",
    )(x2d, x2d, g, w_up, w_up, cw, cw, cb, cb, w_down)


def _ple_body(x_ref, p_ref, g_ref, wg_ref, wp_ref, gf_ref, o_ref, *, final):
    x = x_ref[...]
    h = _rms(x, g_ref[...]).astype(BF16)
    gate = _sigmoid(jnp.dot(h, wg_ref[...], preferred_element_type=F32))
    proj = jnp.dot(p_ref[...].astype(BF16), wp_ref[...], preferred_element_type=F32)
    y = x + gate * proj
    if final:
        y = _rms(y, gf_ref[...])
    o_ref[...] = y


def _ple(x2d, p2d, g, wg, wp, g_final, final):
    m, d = x2d.shape
    tm = min(512, m)
    const = lambda shape: pl.BlockSpec(shape, lambda i: (0, 0), pipeline_mode=pl.Buffered(1))
    return pl.pallas_call(
        functools.partial(_ple_body, final=final),
        grid=(m // tm,),
        in_specs=[pl.BlockSpec((tm, d), lambda i: (i, 0)),
                  pl.BlockSpec((tm, PLE_DIM), lambda i: (i, 0)),
                  pl.BlockSpec((1, d), lambda i: (0, 0)),
                  const((d, d)), const((PLE_DIM, d)),
                  pl.BlockSpec((1, d), lambda i: (0, 0))],
        out_specs=pl.BlockSpec((tm, d), lambda i: (i, 0)),
        out_shape=jax.ShapeDtypeStruct((m, d), F32),
        compiler_params=_params(("parallel",), 40),
        name="ple",
    )(x2d, p2d, g, wg, wp, g_final)


def kernel(x, p, g_mix, w_in, lam_q1, lam_k1, lam_q2, lam_k2, g_subln, w_attn_br,
           conv_w, conv_b, ln_g, ln_b, w_conv_br, w_o, g_ffn, w_up, ffn_conv_w,
           ffn_conv_b, w_down, g_ple, w_ple_gate, w_ple_proj, g_final):
    b, s, d = x.shape
    depth = w_in.shape[0]
    m = b * s
    row = lambda a: a.reshape(1, -1)
    slopes = jnp.exp2(-8.0 * jnp.arange(1, N_HEADS + 1, dtype=F32) / N_HEADS)
    x2d = x.reshape(m, d)
    for l in range(depth):
        lam_init = 0.8 - 0.6 * math.exp(-0.3 * l)
        lam_vecs = jnp.stack([lam_q1[l], lam_k1[l], lam_q2[l], lam_k2[l]]).astype(F32)
        z = _inproj(x2d, row(g_mix[l]), w_in[l].astype(BF16))
        z3 = z.reshape(b, s, IN_COLS)
        attn = _attention(z3, slopes, lam_vecs, row(g_subln[l]), lam_init)
        conv = _convbr(z3, conv_w[l], row(conv_b[l]), row(ln_g[l]), row(ln_b[l]))
        x2d = _merge(x2d, z, attn.reshape(m, ATTN_WIDTH), conv.reshape(m, CONV_CH),
                     w_attn_br[l].astype(BF16), w_conv_br[l].astype(BF16), w_o[l].astype(BF16))
        x2d = _ffn(x2d, s, row(g_ffn[l]), w_up[l].astype(BF16), ffn_conv_w[l],
                   row(ffn_conv_b[l]), w_down[l].astype(BF16))
        x2d = _ple(x2d, p[l].reshape(m, PLE_DIM), row(g_ple[l]), w_ple_gate[l].astype(BF16),
                   w_ple_proj[l].astype(BF16), row(g_final), l == depth - 1)
    return x2d.reshape(b, s, d)
```

```python
import functools
import math

import jax
import jax.numpy as jnp
from jax import lax
from jax.experimental import pallas as pl
from jax.experimental.pallas import tpu as pltpu

D_MODEL = 2048
N_HEADS = 8
HEAD_DIM = 64
V_DIM = 2 * HEAD_DIM
ATTN_WIDTH = N_HEADS * V_DIM
CONV_CH = 1024
CONV_K = 31
D_FF = 5632
FFN_CONV_K = 3
PLE_DIM = 256
EPS = 1e-6
LN_EPS = 1e-5
IN_COLS = 3 * ATTN_WIDTH + 2 * CONV_CH + 2 * D_MODEL

LANES = 128
SUBLANES = 8
BF16_ROWS = 16
KEY_CHUNK = 256
CONV_HALO = 32
FFN_HALO = 16

F32 = jnp.float32
BF16 = jnp.bfloat16
MIB = 1024 * 1024


def _rms(x, g):
    ms = jnp.mean(x * x, axis=-1, keepdims=True)
    return x * lax.rsqrt(ms + EPS) * g


def _sigmoid(x):
    return 1.0 / (1.0 + jnp.exp(-x))


def _params(sem, vmem_mib, flags=None):
    return pltpu.CompilerParams(dimension_semantics=sem, vmem_limit_bytes=vmem_mib * MIB,
                                flags=flags)


def _inproj_body(x_ref, g_ref, w_ref, o_ref, h_ref):
    @pl.when(pl.program_id(1) == 0)
    def _():
        h_ref[...] = _rms(x_ref[...], g_ref[...]).astype(h_ref.dtype)

    o_ref[...] = jnp.dot(h_ref[...], w_ref[...],
                         preferred_element_type=F32).astype(o_ref.dtype)


def _inproj(x2d, g, w):
    m, d = x2d.shape
    n = w.shape[1]
    tm = min(1024, m)
    tn = 1024
    return pl.pallas_call(
        _inproj_body,
        grid=(m // tm, n // tn),
        in_specs=[pl.BlockSpec((tm, d), lambda i, j: (i, 0)),
                  pl.BlockSpec((1, d), lambda i, j: (0, 0)),
                  pl.BlockSpec((d, tn), lambda i, j: (0, j))],
        out_specs=pl.BlockSpec((tm, tn), lambda i, j: (i, j)),
        out_shape=jax.ShapeDtypeStruct((m, n), BF16),
        scratch_shapes=[pltpu.VMEM((tm, d), BF16)],
        compiler_params=_params(("parallel", "arbitrary"), 48),
        name="inproj",
    )(x2d, g, w)


def _pos_columns(sp_ref, h, idx, lane, key_side):
    hi = (idx >> 7).astype(F32)
    lo = (idx & (LANES - 1)).astype(F32)
    out = jnp.zeros(lane.shape, F32)
    for p in range(3):
        sp = sp_ref[h, p]
        if key_side:
            cols = (hi, lo, -float(LANES) * sp, -sp)
        else:
            cols = (float(LANES) * sp, sp, hi, lo)
        for t, val in enumerate(cols):
            out = jnp.where(lane == 4 * p + t, val, out)
    return out.astype(BF16)


def _attn_body(sp_ref, q_ref, k_ref, v_ref, lam_ref, g_ref, o_ref,
               qa_ref, ka_ref, va_ref, s_ref, p_ref, mx_ref, m_ref, acc_ref,
               *, tq, seq, lam_init):
    h = pl.program_id(1)
    qi = pl.program_id(2)
    nt = (((1,), (1,)), ((), ()))

    @pl.when(qi == 0)
    def _():
        def fill(c, carry):
            rows = pl.ds(pl.multiple_of(c * tq, tq), tq)
            lane = lax.broadcasted_iota(jnp.int32, (tq, LANES), 1)
            idx = lax.broadcasted_iota(jnp.int32, (tq, LANES), 0) + c * tq
            ka_ref[rows, pl.ds(0, V_DIM)] = k_ref[rows, :]
            ka_ref[rows, pl.ds(V_DIM, LANES)] = _pos_columns(sp_ref, h, idx, lane, True)
            va_ref[rows, pl.ds(0, V_DIM)] = v_ref[rows, :]
            va_ref[rows, pl.ds(V_DIM, LANES)] = jnp.ones((tq, LANES), BF16)
            return carry
        lax.fori_loop(0, seq // tq, fill, 0)

    q = q_ref[...]
    lane = lax.broadcasted_iota(jnp.int32, q.shape, 1)
    idx = lax.broadcasted_iota(jnp.int32, q.shape, 0) + qi * tq
    qs = q * (HEAD_DIM ** -0.5 * math.log2(math.e))
    zero = jnp.zeros_like(qs)
    qpos = _pos_columns(sp_ref, h, idx, lane, False)
    qa_ref[0, :, pl.ds(0, V_DIM)] = jnp.where(lane < HEAD_DIM, qs, zero)
    qa_ref[1, :, pl.ds(0, V_DIM)] = jnp.where(lane >= HEAD_DIM, qs, zero)
    qa_ref[0, :, pl.ds(V_DIM, LANES)] = qpos
    qa_ref[1, :, pl.ds(V_DIM, LANES)] = qpos

    m_ref[...] = jnp.full(m_ref.shape, -jnp.inf, F32)
    acc_ref[...] = jnp.zeros(acc_ref.shape, F32)

    def tile_rows(kb):
        return pl.ds(pl.multiple_of(kb * tq, tq), tq)

    def scores(n, kb, diagonal):
        s = lax.dot_general(qa_ref[n], ka_ref[tile_rows(kb), :], nt,
                            preferred_element_type=F32)
        if diagonal is not False:
            row = lax.broadcasted_iota(jnp.int32, s.shape, 0)
            col = lax.broadcasted_iota(jnp.int32, s.shape, 1)
            slack = 0 if diagonal is True else jnp.where(diagonal, 0, tq)
            s = jnp.where(col <= row + slack, s, -jnp.inf)
        s_ref[n] = s
        mx = s[:, :LANES]
        for c0 in range(LANES, tq, LANES):
            mx = jnp.maximum(mx, s[:, c0:c0 + LANES])
        mx_ref[n] = mx

    def softmax_accumulate(n, kb):
        m_prev = m_ref[n]
        m_next = jnp.maximum(m_prev, jnp.max(mx_ref[n], axis=1, keepdims=True))
        alpha = jnp.exp2(m_prev - m_next)
        m_ref[n] = m_next
        for c0 in range(0, tq, KEY_CHUNK):
            p_ref[n, :, pl.ds(c0, KEY_CHUNK)] = jnp.exp2(
                (s_ref[n, :, pl.ds(c0, KEY_CHUNK)]
                 - jnp.tile(m_next, (1, KEY_CHUNK // LANES))).astype(BF16))
        acc_ref[n] = jnp.tile(alpha, (1, 2)) * acc_ref[n] + jnp.dot(
            p_ref[n], va_ref[tile_rows(kb), :], preferred_element_type=F32)

    def step(kb, last):
        for n in range(2):
            softmax_accumulate(n, kb)
            scores(n, kb + 1, last)

    scores(0, 0, qi == 0)
    scores(1, 0, qi == 0)

    def body(kb, carry):
        step(kb, False)
        return carry

    lax.fori_loop(0, qi - 1, body, 0)

    @pl.when(qi > 0)
    def _():
        step(qi - 1, True)

    softmax_accumulate(0, qi)
    softmax_accumulate(1, qi)

    lv = lam_ref[...]
    e1 = jnp.exp(jnp.sum(lv[0:1] * lv[1:2], axis=1, keepdims=True))
    e2 = jnp.exp(jnp.sum(lv[2:3] * lv[3:4], axis=1, keepdims=True))
    lam = e1 - e2 + lam_init
    a0 = acc_ref[0]
    a1 = acc_ref[1]
    o = a0[:, :V_DIM] / a0[:, V_DIM:] - lam * (a1[:, :V_DIM] / a1[:, V_DIM:])
    o_ref[...] = (_rms(o, g_ref[...]) * (1.0 - lam_init)).astype(o_ref.dtype)


def _split_bf16(x):
    hi = x.astype(BF16).astype(F32)
    mid = (x - hi).astype(BF16).astype(F32)
    lo = (x - hi - mid).astype(BF16).astype(F32)
    return jnp.stack([hi, mid, lo], axis=-1)


def _attention(z3, slopes, lam_vecs, g_subln, lam_init):
    b, s, _ = z3.shape
    tq = min(1024, s)
    assert s % tq == 0 and s <= LANES * 256
    kcol = ATTN_WIDTH // V_DIM
    vcol = 2 * ATTN_WIDTH // V_DIM
    sp = _split_bf16(slopes * math.log2(math.e))
    once = lambda col: pl.BlockSpec((None, s, V_DIM), lambda bi, h, i: (bi, 0, col + h),
                                    pipeline_mode=pl.Buffered(1))
    return pl.pallas_call(
        functools.partial(_attn_body, tq=tq, seq=s, lam_init=lam_init),
        grid=(b, N_HEADS, s // tq),
        in_specs=[pl.BlockSpec(memory_space=pltpu.SMEM),
                  pl.BlockSpec((None, tq, V_DIM), lambda bi, h, i: (bi, i, h)),
                  once(kcol), once(vcol),
                  pl.BlockSpec((4, HEAD_DIM), lambda bi, h, i: (0, 0)),
                  pl.BlockSpec((1, V_DIM), lambda bi, h, i: (0, 0))],
        out_specs=pl.BlockSpec((None, tq, V_DIM), lambda bi, h, i: (bi, i, h)),
        out_shape=jax.ShapeDtypeStruct((b, s, ATTN_WIDTH), BF16),
        scratch_shapes=[pltpu.VMEM((2, tq, V_DIM + LANES), BF16),
                        pltpu.VMEM((s, V_DIM + LANES), BF16),
                        pltpu.VMEM((s, V_DIM + LANES), BF16),
                        pltpu.VMEM((2, tq, tq), F32),
                        pltpu.VMEM((2, tq, tq), BF16),
                        pltpu.VMEM((2, tq, LANES), F32),
                        pltpu.VMEM((2, tq, LANES), F32),
                        pltpu.VMEM((2, tq, V_DIM + LANES), F32)],
        compiler_params=_params(("parallel", "parallel", "arbitrary"), 56),
        name="attn",
    )(sp, z3, z3, z3, lam_vecs, g_subln)


def _convbr_body(a_ref, g_ref, ah_ref, gh_ref, w_ref, b_ref, lng_ref, lnb_ref, o_ref,
                 u_ref, y_ref, *, ts, rc):
    i = pl.program_id(1)
    uh = ah_ref[...].astype(F32) * _sigmoid(gh_ref[...].astype(F32))
    u_ref[0, pl.ds(0, CONV_HALO), :] = jnp.where(i > 0, uh, 0.0)
    u_ref[0, pl.ds(CONV_HALO, ts), :] = a_ref[...].astype(F32) * _sigmoid(g_ref[...].astype(F32))
    for c in range(CONV_CH // LANES):
        cs = pl.ds(c * LANES, LANES)
        for r0 in range(SUBLANES, ts + CONV_HALO, rc):
            n = min(rc, ts + CONV_HALO - r0)
            u = u_ref[0, pl.ds(r0 - SUBLANES, n + SUBLANES), cs]
            for sh in range(1, SUBLANES):
                u_ref[sh, pl.ds(r0, n), cs] = pltpu.roll(u, sh, axis=0)[SUBLANES:, :]

    first = CONV_HALO - (CONV_K - 1)
    for c in range(CONV_CH // LANES):
        cs = pl.ds(c * LANES, LANES)
        for r0 in range(0, ts, rc):
            parts = [jnp.broadcast_to(b_ref[:, cs], (rc, LANES)), None, None, None]
            for k in range(CONV_K):
                sh = -(first + k) % SUBLANES
                term = w_ref[pl.ds(k, 1), cs] * u_ref[sh, pl.ds(r0 + first + k + sh, rc), cs]
                parts[k % 4] = term if parts[k % 4] is None else parts[k % 4] + term
            y_ref[pl.ds(r0, rc), cs] = (parts[0] + parts[1]) + (parts[2] + parts[3])

    for r0 in range(0, ts, BF16_ROWS):
        rows = pl.ds(r0, BF16_ROWS)
        y = y_ref[rows, :]
        mu = jnp.mean(y, axis=-1, keepdims=True)
        yc = y - mu
        var = jnp.mean(yc * yc, axis=-1, keepdims=True)
        yn = yc * lax.rsqrt(var + LN_EPS) * lng_ref[...] + lnb_ref[...]
        o_ref[rows, :] = (yn * _sigmoid(yn)).astype(o_ref.dtype)


def _convbr(z3, conv_w, conv_b, ln_g, ln_b):
    b, s, _ = z3.shape
    ts = min(512, s)
    rc = 64
    acol = 3 * ATTN_WIDTH // CONV_CH
    gcol = acol + 1
    hb = ts // CONV_HALO
    halo = lambda col: (lambda bi, i: (bi, jnp.maximum(i * hb - 1, 0), col))
    return pl.pallas_call(
        functools.partial(_convbr_body, ts=ts, rc=rc),
        grid=(b, s // ts),
        in_specs=[pl.BlockSpec((None, ts, CONV_CH), lambda bi, i: (bi, i, acol)),
                  pl.BlockSpec((None, ts, CONV_CH), lambda bi, i: (bi, i, gcol)),
                  pl.BlockSpec((None, CONV_HALO, CONV_CH), halo(acol)),
                  pl.BlockSpec((None, CONV_HALO, CONV_CH), halo(gcol)),
                  pl.BlockSpec((CONV_K, CONV_CH), lambda bi, i: (0, 0)),
                  pl.BlockSpec((1, CONV_CH), lambda bi, i: (0, 0)),
                  pl.BlockSpec((1, CONV_CH), lambda bi, i: (0, 0)),
                  pl.BlockSpec((1, CONV_CH), lambda bi, i: (0, 0))],
        out_specs=pl.BlockSpec((None, ts, CONV_CH), lambda bi, i: (bi, i, 0)),
        out_shape=jax.ShapeDtypeStruct((b, s, CONV_CH), BF16),
        scratch_shapes=[pltpu.VMEM((SUBLANES, ts + CONV_HALO, CONV_CH), F32),
                        pltpu.VMEM((ts, CONV_CH), F32)],
        compiler_params=_params(("parallel", "arbitrary"), 48),
        name="convbr",
    )(z3, z3, z3, z3, conv_w, conv_b, ln_g, ln_b)


def _merge_body(x_ref, at_ref, cb_ref, ga0_ref, ga1_ref, gc0_ref, gc1_ref,
                wa_ref, wc_ref, wo_ref, o_ref):
    att = jnp.dot(at_ref[...], wa_ref[...], preferred_element_type=F32)
    cv = jnp.dot(cb_ref[...], wc_ref[...], preferred_element_type=F32)
    ga = jnp.concatenate([ga0_ref[...], ga1_ref[...]], axis=1).astype(F32)
    gc = jnp.concatenate([gc0_ref[...], gc1_ref[...]], axis=1).astype(F32)
    merged = (_sigmoid(ga) * att + _sigmoid(gc) * cv).astype(BF16)
    o_ref[...] = x_ref[...] + jnp.dot(merged, wo_ref[...], preferred_element_type=F32)


def _merge(x2d, z, attn2d, conv2d, wa, wc, wo):
    m, d = x2d.shape
    tm = min(512, m)
    gw = ATTN_WIDTH
    g0 = (3 * ATTN_WIDTH + 2 * CONV_CH) // gw
    gate = lambda col: pl.BlockSpec((tm, gw), lambda i: (i, col))
    const = lambda shape: pl.BlockSpec(shape, lambda i: (0, 0), pipeline_mode=pl.Buffered(1))
    return pl.pallas_call(
        _merge_body,
        grid=(m // tm,),
        in_specs=[pl.BlockSpec((tm, d), lambda i: (i, 0)),
                  pl.BlockSpec((tm, ATTN_WIDTH), lambda i: (i, 0)),
                  pl.BlockSpec((tm, CONV_CH), lambda i: (i, 0)),
                  gate(g0), gate(g0 + 1), gate(g0 + 2), gate(g0 + 3),
                  const((ATTN_WIDTH, d)), const((CONV_CH, d)), const((d, d))],
        out_specs=pl.BlockSpec((tm, d), lambda i: (i, 0)),
        out_shape=jax.ShapeDtypeStruct((m, d), F32),
        compiler_params=_params(("parallel",), 56),
        name="merge",
    )(x2d, attn2d, conv2d, z, z, z, z, wa, wc, wo)


def _ffn_body(x_ref, xh_ref, g_ref, wg_ref, wv_ref, cwg_ref, cwv_ref, cbg_ref, cbv_ref,
              wd_ref, o_ref, h_ref, acc_ref, *, tm, tiles_per_seq):
    i = pl.program_id(0)
    j = pl.program_id(1)

    @pl.when(j == 0)
    def _():
        x = x_ref[...]
        h_ref[pl.ds(FFN_HALO, tm), :] = _rms(x, g_ref[...]).astype(h_ref.dtype)
        hh = _rms(xh_ref[...], g_ref[...])
        h_ref[pl.ds(0, FFN_HALO), :] = jnp.where(i % tiles_per_seq != 0, hh, 0.0).astype(h_ref.dtype)
        acc_ref[...] = x

    h = h_ref[...]

    def conv(w_ref, cw_ref, cb_ref):
        u = jnp.dot(h, w_ref[...], preferred_element_type=F32)
        y = cw_ref[pl.ds(2, 1), :] * u
        y = y + cw_ref[pl.ds(1, 1), :] * pltpu.roll(u, 1, axis=0)
        y = y + cw_ref[pl.ds(0, 1), :] * pltpu.roll(u, 2, axis=0)
        return y[FFN_HALO:, :] + cb_ref[...]

    yg = conv(wg_ref, cwg_ref, cbg_ref)
    yv = conv(wv_ref, cwv_ref, cbv_ref)
    act = (yg * _sigmoid(yg) * yv).astype(BF16)
    acc_ref[...] += jnp.dot(act, wd_ref[...], preferred_element_type=F32)

    @pl.when(j == pl.num_programs(1) - 1)
    def _():
        o_ref[...] = acc_ref[...]


def _ffn(x2d, seq, g, w_up, cw, cb, w_down):
    m, d = x2d.shape
    tm = min(512, seq)
    tf = 512
    nf = D_FF // tf
    hb = tm // FFN_HALO
    return pl.pallas_call(
        functools.partial(_ffn_body, tm=tm, tiles_per_seq=seq // tm),
        grid=(m // tm, nf),
        in_specs=[pl.BlockSpec((tm, d), lambda i, j: (i, 0)),
                  pl.BlockSpec((FFN_HALO, d), lambda i, j: (jnp.maximum(i * hb - 1, 0), 0)),
                  pl.BlockSpec((1, d), lambda i, j: (0, 0)),
                  pl.BlockSpec((d, tf), lambda i, j: (0, j)),
                  pl.BlockSpec((d, tf), lambda i, j: (0, nf + j)),
                  pl.BlockSpec((FFN_CONV_K, tf), lambda i, j: (0, j)),
                  pl.BlockSpec((FFN_CONV_K, tf), lambda i, j: (0, nf + j)),
                  pl.BlockSpec((1, tf), lambda i, j: (0, j)),
                  pl.BlockSpec((1, tf), lambda i, j: (0, nf + j)),
                  pl.BlockSpec((tf, d), lambda i, j: (j, 0))],
        out_specs=pl.BlockSpec((tm, d), lambda i, j: (i, 0)),
        out_shape=jax.ShapeDtypeStruct((m, d), F32),
        scratch_shapes=[pltpu.VMEM((tm + FFN_HALO, d), BF16),
                        pltpu.VMEM((tm, d), F32)],
        compiler_params=_params(("parallel", "arbitrary"), 48),
        name="ffn",
    )(x2d, x2d, g, w_up, w_up, cw, cw, cb, cb, w_down)


def _ple_body(x_ref, p_ref, g_ref, wg_ref, wp_ref, gf_ref, o_ref, *, final):
    x = x_ref[...]
    h = _rms(x, g_ref[...]).astype(BF16)
    gate = _sigmoid(jnp.dot(h, wg_ref[...], preferred_element_type=F32))
    proj = jnp.dot(p_ref[...].astype(BF16), wp_ref[...], preferred_element_type=F32)
    y = x + gate * proj
    if final:
        y = _rms(y, gf_ref[...])
    o_ref[...] = y


def _ple(x2d, p2d, g, wg, wp, g_final, final):
    m, d = x2d.shape
    tm = min(512, m)
    const = lambda shape: pl.BlockSpec(shape, lambda i: (0, 0), pipeline_mode=pl.Buffered(1))
    return pl.pallas_call(
        functools.partial(_ple_body, final=final),
        grid=(m // tm,),
        in_specs=[pl.BlockSpec((tm, d), lambda i: (i, 0)),
                  pl.BlockSpec((tm, PLE_DIM), lambda i: (i, 0)),
                  pl.BlockSpec((1, d), lambda i: (0, 0)),
                  const((d, d)), const((PLE_DIM, d)),
                  pl.BlockSpec((1, d), lambda i: (0, 0))],
        out_specs=pl.BlockSpec((tm, d), lambda i: (i, 0)),
        out_shape=jax.ShapeDtypeStruct((m, d), F32),
        compiler_params=_params(("parallel",), 40),
        name="ple",
    )(x2d, p2d, g, wg, wp, g_final)


def kernel(x, p, g_mix, w_in, lam_q1, lam_k1, lam_q2, lam_k2, g_subln, w_attn_br,
           conv_w, conv_b, ln_g, ln_b, w_conv_br, w_o, g_ffn, w_up, ffn_conv_w,
           ffn_conv_b, w_down, g_ple, w_ple_gate, w_ple_proj, g_final):
    b, s, d = x.shape
    depth = w_in.shape[0]
    m = b * s
    row = lambda a: a.reshape(1, -1)
    slopes = jnp.exp2(-8.0 * jnp.arange(1, N_HEADS + 1, dtype=F32) / N_HEADS)
    x2d = x.reshape(m, d)
    for l in range(depth):
        lam_init = 0.8 - 0.6 * math.exp(-0.3 * l)
        lam_vecs = jnp.stack([lam_q1[l], lam_k1[l], lam_q2[l], lam_k2[l]]).astype(F32)
        z = _inproj(x2d, row(g_mix[l]), w_in[l].astype(BF16))
        z3 = z.reshape(b, s, IN_COLS)
        attn = _attention(z3, slopes, lam_vecs, row(g_subln[l]), lam_init)
        conv = _convbr(z3, conv_w[l], row(conv_b[l]), row(ln_g[l]), row(ln_b[l]))
        x2d = _merge(x2d, z, attn.reshape(m, ATTN_WIDTH), conv.reshape(m, CONV_CH),
                     w_attn_br[l].astype(BF16), w_conv_br[l].astype(BF16), w_o[l].astype(BF16))
        x2d = _ffn(x2d, s, row(g_ffn[l]), w_up[l].astype(BF16), ffn_conv_w[l],
                   row(ffn_conv_b[l]), w_down[l].astype(BF16))
        x2d = _ple(x2d, p[l].reshape(m, PLE_DIM), row(g_ple[l]), w_ple_gate[l].astype(BF16),
                   w_ple_proj[l].astype(BF16), row(g_final), l == depth - 1)
    return x2d.reshape(b, s, d)
```

```python
import functools
import math

import jax
import jax.numpy as jnp
from jax import lax
from jax.experimental import pallas as pl
from jax.experimental.pallas import tpu as pltpu

D_MODEL = 2048
N_HEADS = 8
HEAD_DIM = 64
V_DIM = 2 * HEAD_DIM
ATTN_WIDTH = N_HEADS * V_DIM
CONV_CH = 1024
CONV_K = 31
D_FF = 5632
FFN_CONV_K = 3
PLE_DIM = 256
EPS = 1e-6
LN_EPS = 1e-5
IN_COLS = 3 * ATTN_WIDTH + 2 * CONV_CH + 2 * D_MODEL

LANES = 128
SUBLANES = 8
BF16_ROWS = 16
KEY_CHUNK = 256
CONV_HALO = 32
FFN_HALO = 16

F32 = jnp.float32
BF16 = jnp.bfloat16
MIB = 1024 * 1024


def _rms(x, g):
    ms = jnp.mean(x * x, axis=-1, keepdims=True)
    return x * lax.rsqrt(ms + EPS) * g


def _sigmoid(x):
    return 1.0 / (1.0 + jnp.exp(-x))


def _params(sem, vmem_mib, flags=None):
    return pltpu.CompilerParams(dimension_semantics=sem, vmem_limit_bytes=vmem_mib * MIB,
                                flags=flags)


def _inproj_body(x_ref, g_ref, w_ref, o_ref, h_ref):
    @pl.when(pl.program_id(1) == 0)
    def _():
        h_ref[...] = _rms(x_ref[...], g_ref[...]).astype(h_ref.dtype)

    o_ref[...] = jnp.dot(h_ref[...], w_ref[...],
                         preferred_element_type=F32).astype(o_ref.dtype)


def _inproj(x2d, g, w):
    m, d = x2d.shape
    n = w.shape[1]
    tm = min(1024, m)
    tn = 1024
    return pl.pallas_call(
        _inproj_body,
        grid=(m // tm, n // tn),
        in_specs=[pl.BlockSpec((tm, d), lambda i, j: (i, 0)),
                  pl.BlockSpec((1, d), lambda i, j: (0, 0)),
                  pl.BlockSpec((d, tn), lambda i, j: (0, j))],
        out_specs=pl.BlockSpec((tm, tn), lambda i, j: (i, j)),
        out_shape=jax.ShapeDtypeStruct((m, n), BF16),
        scratch_shapes=[pltpu.VMEM((tm, d), BF16)],
        compiler_params=_params(("parallel", "arbitrary"), 48),
        name="inproj",
    )(x2d, g, w)


def _pos_columns(sp_ref, h, idx, lane, key_side):
    hi = (idx >> 7).astype(F32)
    lo = (idx & (LANES - 1)).astype(F32)
    out = jnp.zeros(lane.shape, F32)
    for p in range(3):
        sp = sp_ref[h, p]
        if key_side:
            cols = (hi, lo, -float(LANES) * sp, -sp)
        else:
            cols = (float(LANES) * sp, sp, hi, lo)
        for t, val in enumerate(cols):
            out = jnp.where(lane == 4 * p + t, val, out)
    return out.astype(BF16)


def _attn_body(sp_ref, q_ref, k_ref, v_ref, lam_ref, g_ref, o_ref,
               qa_ref, ka_ref, va_ref, s_ref, mx_ref, m_ref, acc_ref,
               *, tq, seq, lam_init):
    h = pl.program_id(1)
    qi = pl.program_id(2)
    nt = (((1,), (1,)), ((), ()))

    @pl.when(qi == 0)
    def _():
        def fill(c, carry):
            rows = pl.ds(pl.multiple_of(c * tq, tq), tq)
            lane = lax.broadcasted_iota(jnp.int32, (tq, LANES), 1)
            idx = lax.broadcasted_iota(jnp.int32, (tq, LANES), 0) + c * tq
            ka_ref[rows, pl.ds(0, V_DIM)] = k_ref[rows, :]
            ka_ref[rows, pl.ds(V_DIM, LANES)] = _pos_columns(sp_ref, h, idx, lane, True)
            va_ref[rows, pl.ds(0, V_DIM)] = v_ref[rows, :]
            va_ref[rows, pl.ds(V_DIM, LANES)] = jnp.ones((tq, LANES), BF16)
            return carry
        lax.fori_loop(0, seq // tq, fill, 0)

    q = q_ref[...]
    lane = lax.broadcasted_iota(jnp.int32, q.shape, 1)
    idx = lax.broadcasted_iota(jnp.int32, q.shape, 0) + qi * tq
    qs = q * (HEAD_DIM ** -0.5 * math.log2(math.e))
    zero = jnp.zeros_like(qs)
    qpos = _pos_columns(sp_ref, h, idx, lane, False)
    qa_ref[0, :, pl.ds(0, V_DIM)] = jnp.where(lane < HEAD_DIM, qs, zero)
    qa_ref[1, :, pl.ds(0, V_DIM)] = jnp.where(lane >= HEAD_DIM, qs, zero)
    qa_ref[0, :, pl.ds(V_DIM, LANES)] = qpos
    qa_ref[1, :, pl.ds(V_DIM, LANES)] = qpos

    m_ref[...] = jnp.full(m_ref.shape, -jnp.inf, F32)
    acc_ref[...] = jnp.zeros(acc_ref.shape, F32)

    def score_chunk(n, kb, c0, diagonal):
        keys = pl.ds(pl.multiple_of(kb * tq, tq) + c0, KEY_CHUNK)
        s = lax.dot_general(qa_ref[n], ka_ref[keys, :], nt, preferred_element_type=F32)
        if diagonal is not False:
            row = lax.broadcasted_iota(jnp.int32, s.shape, 0)
            col = lax.broadcasted_iota(jnp.int32, s.shape, 1) + c0
            slack = 0 if diagonal is True else jnp.where(diagonal, 0, tq)
            s = jnp.where(col <= row + slack, s, -jnp.inf)
        s_ref[n, :, pl.ds(c0, KEY_CHUNK)] = s
        return s

    def lane_max(mx, s):
        for l0 in range(0, KEY_CHUNK, LANES):
            part = s[:, l0:l0 + LANES]
            mx = part if mx is None else jnp.maximum(mx, part)
        return mx

    def scores(n, kb, diagonal):
        mx = None
        for c0 in range(0, tq, KEY_CHUNK):
            mx = lane_max(mx, score_chunk(n, kb, c0, diagonal))
        mx_ref[n] = mx

    def rescale(n):
        m_prev = m_ref[n]
        m_next = jnp.maximum(m_prev, jnp.max(mx_ref[n], axis=1, keepdims=True))
        m_ref[n] = m_next
        return m_next, jnp.tile(jnp.exp2(m_prev - m_next), (1, 2)) * acc_ref[n]

    def prob_chunk(n, m_next, c0):
        return (s_ref[n, :, pl.ds(c0, KEY_CHUNK)]
                - jnp.tile(m_next, (1, KEY_CHUNK // LANES))).astype(BF16)

    def value_chunk(kb, c0):
        return va_ref[pl.ds(pl.multiple_of(kb * tq, tq) + c0, KEY_CHUNK), :]

    def softmax_accumulate(n, kb):
        m_next, acc = rescale(n)
        for c0 in range(0, tq, KEY_CHUNK):
            acc = acc + jnp.dot(jnp.exp2(prob_chunk(n, m_next, c0)), value_chunk(kb, c0),
                                preferred_element_type=F32)
        acc_ref[n] = acc

    def step(kb, last):
        for n in range(2):
            m_next, acc = rescale(n)
            mx = None
            for c0 in range(0, tq, KEY_CHUNK):
                x = prob_chunk(n, m_next, c0)
                acc = acc + jnp.dot(jnp.exp2(x), value_chunk(kb, c0), preferred_element_type=F32)
                mx = lane_max(mx, score_chunk(n, kb + 1, c0, last))
            acc_ref[n] = acc
            mx_ref[n] = mx

    scores(0, 0, qi == 0)
    scores(1, 0, qi == 0)

    def body(kb, carry):
        step(kb, False)
        return carry

    lax.fori_loop(0, qi - 1, body, 0)

    @pl.when(qi > 0)
    def _():
        step(qi - 1, True)

    softmax_accumulate(0, qi)
    softmax_accumulate(1, qi)

    lv = lam_ref[...]
    e1 = jnp.exp(jnp.sum(lv[0:1] * lv[1:2], axis=1, keepdims=True))
    e2 = jnp.exp(jnp.sum(lv[2:3] * lv[3:4], axis=1, keepdims=True))
    lam = e1 - e2 + lam_init
    a0 = acc_ref[0]
    a1 = acc_ref[1]
    o = a0[:, :V_DIM] / a0[:, V_DIM:] - lam * (a1[:, :V_DIM] / a1[:, V_DIM:])
    o_ref[...] = (_rms(o, g_ref[...]) * (1.0 - lam_init)).astype(o_ref.dtype)


def _split_bf16(x):
    hi = x.astype(BF16).astype(F32)
    mid = (x - hi).astype(BF16).astype(F32)
    lo = (x - hi - mid).astype(BF16).astype(F32)
    return jnp.stack([hi, mid, lo], axis=-1)


def _attention(z3, slopes, lam_vecs, g_subln, lam_init):
    b, s, _ = z3.shape
    tq = min(1024, s)
    assert s % tq == 0 and s <= LANES * 256
    kcol = ATTN_WIDTH // V_DIM
    vcol = 2 * ATTN_WIDTH // V_DIM
    sp = _split_bf16(slopes * math.log2(math.e))
    once = lambda col: pl.BlockSpec((None, s, V_DIM), lambda bi, h, i: (bi, 0, col + h))
    return pl.pallas_call(
        functools.partial(_attn_body, tq=tq, seq=s, lam_init=lam_init),
        grid=(b, N_HEADS, s // tq),
        in_specs=[pl.BlockSpec(memory_space=pltpu.SMEM),
                  pl.BlockSpec((None, tq, V_DIM), lambda bi, h, i: (bi, i, h)),
                  once(kcol), once(vcol),
                  pl.BlockSpec((4, HEAD_DIM), lambda bi, h, i: (0, 0)),
                  pl.BlockSpec((1, V_DIM), lambda bi, h, i: (0, 0))],
        out_specs=pl.BlockSpec((None, tq, V_DIM), lambda bi, h, i: (bi, i, h)),
        out_shape=jax.ShapeDtypeStruct((b, s, ATTN_WIDTH), BF16),
        scratch_shapes=[pltpu.VMEM((2, tq, V_DIM + LANES), BF16),
                        pltpu.VMEM((s, V_DIM + LANES), BF16),
                        pltpu.VMEM((s, V_DIM + LANES), BF16),
                        pltpu.VMEM((2, tq, tq), F32),
                        pltpu.VMEM((2, tq, LANES), F32),
                        pltpu.VMEM((2, tq, LANES), F32),
                        pltpu.VMEM((2, tq, V_DIM + LANES), F32)],
        compiler_params=_params(("parallel", "parallel", "arbitrary"), 56),
        name="attn",
    )(sp, z3, z3, z3, lam_vecs, g_subln)


def _convbr_body(a_ref, g_ref, ah_ref, gh_ref, w_ref, b_ref, lng_ref, lnb_ref, o_ref,
                 u_ref, y_ref, *, ts, rc):
    i = pl.program_id(1)
    uh = ah_ref[...].astype(F32) * _sigmoid(gh_ref[...].astype(F32))
    u_ref[0, pl.ds(0, CONV_HALO), :] = jnp.where(i > 0, uh, 0.0)
    u_ref[0, pl.ds(CONV_HALO, ts), :] = a_ref[...].astype(F32) * _sigmoid(g_ref[...].astype(F32))
    for c in range(CONV_CH // LANES):
        cs = pl.ds(c * LANES, LANES)
        for r0 in range(SUBLANES, ts + CONV_HALO, rc):
            n = min(rc, ts + CONV_HALO - r0)
            u = u_ref[0, pl.ds(r0 - SUBLANES, n + SUBLANES), cs]
            for sh in range(1, SUBLANES):
                u_ref[sh, pl.ds(r0, n), cs] = pltpu.roll(u, sh, axis=0)[SUBLANES:, :]

    first = CONV_HALO - (CONV_K - 1)
    for c in range(CONV_CH // LANES):
        cs = pl.ds(c * LANES, LANES)
        for r0 in range(0, ts, rc):
            parts = [jnp.broadcast_to(b_ref[:, cs], (rc, LANES)), None, None, None]
            for k in range(CONV_K):
                sh = -(first + k) % SUBLANES
                term = w_ref[pl.ds(k, 1), cs] * u_ref[sh, pl.ds(r0 + first + k + sh, rc), cs]
                parts[k % 4] = term if parts[k % 4] is None else parts[k % 4] + term
            y_ref[pl.ds(r0, rc), cs] = (parts[0] + parts[1]) + (parts[2] + parts[3])

    for r0 in range(0, ts, BF16_ROWS):
        rows = pl.ds(r0, BF16_ROWS)
        y = y_ref[rows, :]
        mu = jnp.mean(y, axis=-1, keepdims=True)
        yc = y - mu
        var = jnp.mean(yc * yc, axis=-1, keepdims=True)
        yn = yc * lax.rsqrt(var + LN_EPS) * lng_ref[...] + lnb_ref[...]
        o_ref[rows, :] = (yn * _sigmoid(yn)).astype(o_ref.dtype)


def _convbr(z3, conv_w, conv_b, ln_g, ln_b):
    b, s, _ = z3.shape
    ts = min(512, s)
    rc = 64
    acol = 3 * ATTN_WIDTH // CONV_CH
    gcol = acol + 1
    hb = ts // CONV_HALO
    halo = lambda col: (lambda bi, i: (bi, jnp.maximum(i * hb - 1, 0), col))
    return pl.pallas_call(
        functools.partial(_convbr_body, ts=ts, rc=rc),
        grid=(b, s // ts),
        in_specs=[pl.BlockSpec((None, ts, CONV_CH), lambda bi, i: (bi, i, acol)),
                  pl.BlockSpec((None, ts, CONV_CH), lambda bi, i: (bi, i, gcol)),
                  pl.BlockSpec((None, CONV_HALO, CONV_CH), halo(acol)),
                  pl.BlockSpec((None, CONV_HALO, CONV_CH), halo(gcol)),
                  pl.BlockSpec((CONV_K, CONV_CH), lambda bi, i: (0, 0)),
                  pl.BlockSpec((1, CONV_CH), lambda bi, i: (0, 0)),
                  pl.BlockSpec((1, CONV_CH), lambda bi, i: (0, 0)),
                  pl.BlockSpec((1, CONV_CH), lambda bi, i: (0, 0))],
        out_specs=pl.BlockSpec((None, ts, CONV_CH), lambda bi, i: (bi, i, 0)),
        out_shape=jax.ShapeDtypeStruct((b, s, CONV_CH), BF16),
        scratch_shapes=[pltpu.VMEM((SUBLANES, ts + CONV_HALO, CONV_CH), F32),
                        pltpu.VMEM((ts, CONV_CH), F32)],
        compiler_params=_params(("parallel", "arbitrary"), 48),
        name="convbr",
    )(z3, z3, z3, z3, conv_w, conv_b, ln_g, ln_b)


def _merge_body(x_ref, at_ref, cb_ref, ga0_ref, ga1_ref, gc0_ref, gc1_ref,
                wa_ref, wc_ref, wo_ref, o_ref):
    att = jnp.dot(at_ref[...], wa_ref[...], preferred_element_type=F32)
    cv = jnp.dot(cb_ref[...], wc_ref[...], preferred_element_type=F32)
    ga = jnp.concatenate([ga0_ref[...], ga1_ref[...]], axis=1).astype(F32)
    gc = jnp.concatenate([gc0_ref[...], gc1_ref[...]], axis=1).astype(F32)
    merged = (_sigmoid(ga) * att + _sigmoid(gc) * cv).astype(BF16)
    o_ref[...] = x_ref[...] + jnp.dot(merged, wo_ref[...], preferred_element_type=F32)


def _merge(x2d, z, attn2d, conv2d, wa, wc, wo):
    m, d = x2d.shape
    tm = min(512, m)
    gw = ATTN_WIDTH
    g0 = (3 * ATTN_WIDTH + 2 * CONV_CH) // gw
    gate = lambda col: pl.BlockSpec((tm, gw), lambda i: (i, col))
    const = lambda shape: pl.BlockSpec(shape, lambda i: (0, 0), pipeline_mode=pl.Buffered(1))
    return pl.pallas_call(
        _merge_body,
        grid=(m // tm,),
        in_specs=[pl.BlockSpec((tm, d), lambda i: (i, 0)),
                  pl.BlockSpec((tm, ATTN_WIDTH), lambda i: (i, 0)),
                  pl.BlockSpec((tm, CONV_CH), lambda i: (i, 0)),
                  gate(g0), gate(g0 + 1), gate(g0 + 2), gate(g0 + 3),
                  const((ATTN_WIDTH, d)), const((CONV_CH, d)), const((d, d))],
        out_specs=pl.BlockSpec((tm, d), lambda i: (i, 0)),
        out_shape=jax.ShapeDtypeStruct((m, d), F32),
        compiler_params=_params(("parallel",), 56),
        name="merge",
    )(x2d, attn2d, conv2d, z, z, z, z, wa, wc, wo)


def _ffn_body(x_ref, xh_ref, g_ref, wg_ref, wv_ref, cwg_ref, cwv_ref, cbg_ref, cbv_ref,
              wd_ref, o_ref, h_ref, acc_ref, *, tm, tiles_per_seq):
    i = pl.program_id(0)
    j = pl.program_id(1)

    @pl.when(j == 0)
    def _():
        x = x_ref[...]
        h_ref[pl.ds(FFN_HALO, tm), :] = _rms(x, g_ref[...]).astype(h_ref.dtype)
        hh = _rms(xh_ref[...], g_ref[...])
        h_ref[pl.ds(0, FFN_HALO), :] = jnp.where(i % tiles_per_seq != 0, hh, 0.0).astype(h_ref.dtype)
        acc_ref[...] = x

    h = h_ref[...]

    def conv(w_ref, cw_ref, cb_ref):
        u = jnp.dot(h, w_ref[...], preferred_element_type=F32)
        y = cw_ref[pl.ds(2, 1), :] * u
        y = y + cw_ref[pl.ds(1, 1), :] * pltpu.roll(u, 1, axis=0)
        y = y + cw_ref[pl.ds(0, 1), :] * pltpu.roll(u, 2, axis=0)
        return y[FFN_HALO:, :] + cb_ref[...]

    yg = conv(wg_ref, cwg_ref, cbg_ref)
    yv = conv(wv_ref, cwv_ref, cbv_ref)
    act = (yg * _sigmoid(yg) * yv).astype(BF16)
    acc_ref[...] += jnp.dot(act, wd_ref[...], preferred_element_type=F32)

    @pl.when(j == pl.num_programs(1) - 1)
    def _():
        o_ref[...] = acc_ref[...]


def _ffn(x2d, seq, g, w_up, cw, cb, w_down):
    m, d = x2d.shape
    tm = min(512, seq)
    tf = 512
    nf = D_FF // tf
    hb = tm // FFN_HALO
    return pl.pallas_call(
        functools.partial(_ffn_body, tm=tm, tiles_per_seq=seq // tm),
        grid=(m // tm, nf),
        in_specs=[pl.BlockSpec((tm, d), lambda i, j: (i, 0)),
                  pl.BlockSpec((FFN_HALO, d), lambda i, j: (jnp.maximum(i * hb - 1, 0), 0)),
                  pl.BlockSpec((1, d), lambda i, j: (0, 0)),
                  pl.BlockSpec((d, tf), lambda i, j: (0, j)),
                  pl.BlockSpec((d, tf), lambda i, j: (0, nf + j)),
                  pl.BlockSpec((FFN_CONV_K, tf), lambda i, j: (0, j)),
                  pl.BlockSpec((FFN_CONV_K, tf), lambda i, j: (0, nf + j)),
                  pl.BlockSpec((1, tf), lambda i, j: (0, j)),
                  pl.BlockSpec((1, tf), lambda i, j: (0, nf + j)),
                  pl.BlockSpec((tf, d), lambda i, j: (j, 0))],
        out_specs=pl.BlockSpec((tm, d), lambda i, j: (i, 0)),
        out_shape=jax.ShapeDtypeStruct((m, d), F32),
        scratch_shapes=[pltpu.VMEM((tm + FFN_HALO, d), BF16),
                        pltpu.VMEM((tm, d), F32)],
        compiler_params=_params(("parallel", "arbitrary"), 48),
        name="ffn",
    )(x2d, x2d, g, w_up, w_up, cw, cw, cb, cb, w_down)


def _ple_body(x_ref, p_ref, g_ref, wg_ref, wp_ref, gf_ref, o_ref, *, final):
    x = x_ref[...]
    h = _rms(x, g_ref[...]).astype(BF16)
    gate = _sigmoid(jnp.dot(h, wg_ref[...], preferred_element_type=F32))
    proj = jnp.dot(p_ref[...].astype(BF16), wp_ref[...], preferred_element_type=F32)
    y = x + gate * proj
    if final:
        y = _rms(y, gf_ref[...])
    o_ref[...] = y


def _ple(x2d, p2d, g, wg, wp, g_final, final):
    m, d = x2d.shape
    tm = min(512, m)
    const = lambda shape: pl.BlockSpec(shape, lambda i: (0, 0), pipeline_mode=pl.Buffered(1))
    return pl.pallas_call(
        functools.partial(_ple_body, final=final),
        grid=(m // tm,),
        in_specs=[pl.BlockSpec((tm, d), lambda i: (i, 0)),
                  pl.BlockSpec((tm, PLE_DIM), lambda i: (i, 0)),
                  pl.BlockSpec((1, d), lambda i: (0, 0)),
                  const((d, d)), const((PLE_DIM, d)),
                  pl.BlockSpec((1, d), lambda i: (0, 0))],
        out_specs=pl.BlockSpec((tm, d), lambda i: (i, 0)),
        out_shape=jax.ShapeDtypeStruct((m, d), F32),
        compiler_params=_params(("parallel",), 40),
        name="ple",
    )(x2d, p2d, g, wg, wp, g_final)


def kernel(x, p, g_mix, w_in, lam_q1, lam_k1, lam_q2, lam_k2, g_subln, w_attn_br,
           conv_w, conv_b, ln_g, ln_b, w_conv_br, w_o, g_ffn, w_up, ffn_conv_w,
           ffn_conv_b, w_down, g_ple, w_ple_gate, w_ple_proj, g_final):
    b, s, d = x.shape
    depth = w_in.shape[0]
    m = b * s
    row = lambda a: a.reshape(1, -1)
    slopes = jnp.exp2(-8.0 * jnp.arange(1, N_HEADS + 1, dtype=F32) / N_HEADS)
    x2d = x.reshape(m, d)
    for l in range(depth):
        lam_init = 0.8 - 0.6 * math.exp(-0.3 * l)
        lam_vecs = jnp.stack([lam_q1[l], lam_k1[l], lam_q2[l], lam_k2[l]]).astype(F32)
        z = _inproj(x2d, row(g_mix[l]), w_in[l].astype(BF16))
        z3 = z.reshape(b, s, IN_COLS)
        attn = _attention(z3, slopes, lam_vecs, row(g_subln[l]), lam_init)
        conv = _convbr(z3, conv_w[l], row(conv_b[l]), row(ln_g[l]), row(ln_b[l]))
        x2d = _merge(x2d, z, attn.reshape(m, ATTN_WIDTH), conv.reshape(m, CONV_CH),
                     w_attn_br[l].astype(BF16), w_conv_br[l].astype(BF16), w_o[l].astype(BF16))
        x2d = _ffn(x2d, s, row(g_ffn[l]), w_up[l].astype(BF16), ffn_conv_w[l],
                   row(ffn_conv_b[l]), w_down[l].astype(BF16))
        x2d = _ple(x2d, p[l].reshape(m, PLE_DIM), row(g_ple[l]), w_ple_gate[l].astype(BF16),
                   w_ple_proj[l].astype(BF16), row(g_final), l == depth - 1)
    return x2d.reshape(b, s, d)
```
